```python
import numpy as np
import jax
import jax.numpy as jnp
from jax import lax


D_MODEL = 1024
BATCH = 2
SEQ = 8192
DEPTH = 4

N_MIXERS = 3
HEAD_DIM = 64
N_HEADS = D_MODEL // HEAD_DIM
Q_BLOCK = 128
NSA_KV_GROUPS = 4
NSA_CMP_LEN = 32
NSA_CMP_STRIDE = 16
NSA_CMP_HIDDEN = 256
NSA_SEL_LEN = 64
NSA_TOP_N = 16
NSA_WINDOW = 512
NSA_IN_DIM = N_HEADS * HEAD_DIM + 6 * NSA_KV_GROUPS * HEAD_DIM + 3 * N_HEADS
SB_IN_DIM = 3 * N_HEADS * HEAD_DIM
SWA_KV_HEADS = 2
SWA_WINDOW = 128
SWA_IN_DIM = (N_HEADS + 2 * SWA_KV_HEADS) * HEAD_DIM
D_FF = 2816
CONV_WIDTH = 3
RMS_EPS = 1e-6
NEG_INF = -1e30
SEL_FORCE = 1e4
N_NSA_LAYERS = (DEPTH + 2) // 3
N_SB_LAYERS = (DEPTH + 1) // 3
N_SWA_LAYERS = DEPTH // 3

kernel_name = "hybrid_nsa_stickbreak_swasink_convffn"


def rms_norm(x, g):
    x32 = x.astype(jnp.float32)
    y = x32 * lax.rsqrt(jnp.mean(x32 * x32, axis=-1, keepdims=True) + RMS_EPS)
    return (y * g.astype(jnp.float32)).astype(x.dtype)


def alibi_slopes(n_heads):
    return jnp.asarray(2.0 ** (-8.0 * np.arange(1, n_heads + 1) / n_heads), dtype=jnp.float32)


def masked_softmax(s, mask, axis):
    s = jnp.where(mask, s, NEG_INF)
    m = jnp.max(s, axis=axis, keepdims=True)
    e = jnp.where(mask, jnp.exp(s - m), 0.0)
    return e / jnp.maximum(jnp.sum(e, axis=axis, keepdims=True), 1e-30)


def nsa_mixer(h, w_in, cmp_pos, cmp_w1, cmp_w2, w_out):
    B, S, _ = h.shape
    H, G, dh = N_HEADS, NSA_KV_GROUPS, HEAD_DIM
    R = H // G
    W = NSA_WINDOW
    n_q = S // Q_BLOCK
    f32 = jnp.float32
    scale = dh ** -0.5
    hd, kd = H * dh, G * dh
    cuts = [hd + c * kd for c in range(7)]
    q, k_cmp, v_cmp, k_sel, v_sel, k_win, v_win, g_logit = jnp.split(h @ w_in, cuts, axis=-1)

    n_cmp = (S - NSA_CMP_LEN) // NSA_CMP_STRIDE + 1
    cmp_start = np.arange(n_cmp) * NSA_CMP_STRIDE
    cmp_idx = cmp_start[:, None] + np.arange(NSA_CMP_LEN)[None, :]

    def compress(u, c):
        blocks = u.reshape(B, S, G, dh)[:, cmp_idx] + cmp_pos[c][None, None, :, None, :]
        flat = blocks.transpose(0, 1, 3, 2, 4).reshape(B, n_cmp, G, NSA_CMP_LEN * dh)
        return jax.nn.silu(flat @ cmp_w1[c]) @ cmp_w2[c]

    ck = compress(k_cmp, 0)
    cv = compress(v_cmp, 1)
    cmp_end = jnp.asarray(cmp_idx[:, -1], dtype=jnp.int32)

    n_sel = S // NSA_SEL_LEN
    sel_start = np.arange(n_sel) * NSA_SEL_LEN
    overlap = (cmp_start[:, None] < sel_start[None, :] + NSA_SEL_LEN) & (cmp_start[:, None] + NSA_CMP_LEN > sel_start[None, :])
    sel_map = jnp.asarray(overlap.astype(np.float32))
    n_top = min(NSA_TOP_N, n_sel)
    ks_blk = k_sel.reshape(B, n_sel, NSA_SEL_LEN, G, dh).transpose(0, 3, 1, 2, 4)
    vs_blk = v_sel.reshape(B, n_sel, NSA_SEL_LEN, G, dh).transpose(0, 3, 1, 2, 4)
    b_ix = jnp.arange(B)[:, None, None, None]
    g_ix = jnp.arange(G)[None, :, None, None]

    pad = ((0, 0), (W, 0), (0, 0), (0, 0))
    kw_pad = jnp.pad(k_win.reshape(B, S, G, dh), pad)
    vw_pad = jnp.pad(v_win.reshape(B, S, G, dh), pad)

    gates = jax.nn.sigmoid(g_logit.astype(f32)).astype(h.dtype)
    gates = gates.reshape(B, n_q, Q_BLOCK, G, R, 3).transpose(1, 0, 2, 3, 4, 5)
    qb_all = q.reshape(B, n_q, Q_BLOCK, G, R, dh).transpose(1, 0, 2, 3, 4, 5)
    slopes = alibi_slopes(H).reshape(G, R)

    def block(args):
        qb, gb, i = args
        t = i * Q_BLOCK + jnp.arange(Q_BLOCK)
        dist_c = (t[:, None] - cmp_end[None, :]).astype(f32)
        s_c = jnp.einsum('btgrd,bngd->bgrtn', qb, ck).astype(f32) * scale - slopes[:, :, None, None] * dist_c
        p_cmp = masked_softmax(s_c, dist_c >= 0, -1)
        o_cmp = jnp.einsum('bgrtn,bngd->btgrd', p_cmp.astype(cv.dtype), cv)
        imp = jnp.einsum('bgrtn,nj->bgtj', p_cmp, sel_map)
        blk = jnp.arange(n_sel)[None, :]
        cur = (t // NSA_SEL_LEN)[:, None]
        forced = (blk == 0) | (blk == cur) | (blk == cur - 1)
        causal_blk = blk * NSA_SEL_LEN <= t[:, None]
        imp = jnp.where(forced, SEL_FORCE, jnp.where(causal_blk, imp, NEG_INF))
        _, sel_idx = lax.top_k(imp, n_top)
        kg = ks_blk[b_ix, g_ix, sel_idx]
        vg = vs_blk[b_ix, g_ix, sel_idx]
        key_pos = sel_idx[..., None] * NSA_SEL_LEN + jnp.arange(NSA_SEL_LEN)
        dist_s = (t[None, None, :, None, None] - key_pos).astype(f32)
        s_s = jnp.einsum('btgrd,bgtnld->bgrtnl', qb, kg).astype(f32) * scale
        s_s = s_s - slopes[None, :, :, None, None, None] * dist_s[:, :, None]
        p_s = masked_softmax(s_s, (dist_s >= 0)[:, :, None], (-2, -1))
        o_sel = jnp.einsum('bgrtnl,bgtnld->btgrd', p_s.astype(vg.dtype), vg)
        kw = lax.dynamic_slice_in_dim(kw_pad, i * Q_BLOCK, W + Q_BLOCK, axis=1)
        vw = lax.dynamic_slice_in_dim(vw_pad, i * Q_BLOCK, W + Q_BLOCK, axis=1)
        s_pos = i * Q_BLOCK - W + jnp.arange(W + Q_BLOCK)
        dist_w = (t[:, None] - s_pos[None, :]).astype(f32)
        mask_w = (dist_w >= 0) & (dist_w < W) & (s_pos[None, :] >= 0)
        s_w = jnp.einsum('btgrd,blgd->bgrtl', qb, kw).astype(f32) * scale - slopes[:, :, None, None] * dist_w
        p_w = masked_softmax(s_w, mask_w, -1)
        o_win = jnp.einsum('bgrtl,blgd->btgrd', p_w.astype(vw.dtype), vw)
        return gb[..., 0:1] * o_cmp + gb[..., 1:2] * o_sel + gb[..., 2:3] * o_win

    out = lax.map(block, (qb_all, gates, jnp.arange(n_q)))
    out = out.transpose(1, 0, 2, 3, 4, 5).reshape(B, S, hd)
    return out @ w_out


def stick_breaking_mixer(h, w_in, w_out):
    B, S, _ = h.shape
    H, dh = N_HEADS, HEAD_DIM
    n_q = S // Q_BLOCK
    scale = dh ** -0.5
    q, k, v = jnp.split(h @ w_in, 3, axis=-1)
    qb_all = q.reshape(B, n_q, Q_BLOCK, H, dh).transpose(1, 0, 3, 2, 4)
    k = k.reshape(B, S, H, dh)
    v = v.reshape(B, S, H, dh)
    s_pos = jnp.arange(S)

    def block(args):
        qb, i = args
        t = i * Q_BLOCK + jnp.arange(Q_BLOCK)
        z = jnp.einsum('bhtd,bshd->bhts', qb, k).astype(jnp.float32) * scale
        mask = s_pos[None, :] < t[:, None]
        log_1m = jnp.where(mask, jax.nn.log_sigmoid(-z), 0.0)
        log_a = jax.nn.log_sigmoid(z) + lax.cumsum(log_1m, axis=3, reverse=True) - log_1m
        a = jnp.where(mask, jnp.exp(log_a), 0.0)
        return jnp.einsum('bhts,bshd->bthd', a.astype(v.dtype), v)

    out = lax.map(block, (qb_all, jnp.arange(n_q)))
    out = out.transpose(1, 0, 2, 3, 4).reshape(B, S, H * dh)
    return out @ w_out


def swa_sink_mixer(h, w_in, sinks, w_out):
    B, S, _ = h.shape
    H, G, dh = N_HEADS, SWA_KV_HEADS, HEAD_DIM
    R = H // G
    W = SWA_WINDOW
    n_q = S // Q_BLOCK
    f32 = jnp.float32
    scale = dh ** -0.5
    q, k, v = jnp.split(h @ w_in, [H * dh, H * dh + G * dh], axis=-1)
    qb_all = q.reshape(B, n_q, Q_BLOCK, G, R, dh).transpose(1, 0, 2, 3, 4, 5)
    pad = ((0, 0), (W, 0), (0, 0), (0, 0))
    k_pad = jnp.pad(k.reshape(B, S, G, dh), pad)
    v_pad = jnp.pad(v.reshape(B, S, G, dh), pad)
    slopes = alibi_slopes(H).reshape(G, R)
    sink = sinks.astype(f32).reshape(G, R)[None, :, :, None, None]

    def block(args):
        qb, i = args
        t = i * Q_BLOCK + jnp.arange(Q_BLOCK)
        kw = lax.dynamic_slice_in_dim(k_pad, i * Q_BLOCK, W + Q_BLOCK, axis=1)
        vw = lax.dynamic_slice_in_dim(v_pad, i * Q_BLOCK, W + Q_BLOCK, axis=1)
        s_pos = i * Q_BLOCK - W + jnp.arange(W + Q_BLOCK)
        dist = (t[:, None] - s_pos[None, :]).astype(f32)
        mask = (dist >= 0) & (dist < W) & (s_pos[None, :] >= 0)
        s = jnp.einsum('btgrd,blgd->bgrtl', qb, kw).astype(f32) * scale - slopes[:, :, None, None] * dist
        s = jnp.where(mask, s, NEG_INF)
        m = jnp.maximum(jnp.max(s, axis=-1, keepdims=True), sink)
        e = jnp.where(mask, jnp.exp(s - m), 0.0)
        p = e / (jnp.sum(e, axis=-1, keepdims=True) + jnp.exp(sink - m))
        return jnp.einsum('bgrtl,blgd->btgrd', p.astype(vw.dtype), vw)

    out = lax.map(block, (qb_all, jnp.arange(n_q)))
    out = out.transpose(1, 0, 2, 3, 4, 5).reshape(B, S, H * dh)
    return out @ w_out


def conv_ffn(h, w_up, conv_w, conv_b, w_down):
    u = h @ w_up
    C = u.shape[-1]
    u = lax.conv_general_dilated(
        u, conv_w[:, None, :].astype(u.dtype), window_strides=(1,),
        padding=[(CONV_WIDTH - 1, 0)], dimension_numbers=('NWC', 'WIO', 'NWC'),
        feature_group_count=C) + conv_b
    gate, val = jnp.split(u, 2, axis=-1)
    return (jax.nn.silu(gate) * val) @ w_down


def setup_inputs(seed: int = 0) -> dict:
    key = jax.random.key(seed)
    ks = jax.random.split(key, 20)
    f32 = jnp.float32
    hd = N_HEADS * HEAD_DIM

    def nrm(k, shape, fan_in):
        return jax.random.normal(k, shape, f32) * fan_in ** -0.5

    return {
        'x': jax.random.normal(ks[0], (BATCH, SEQ, D_MODEL), f32),
        'norm_g': 1.0 + 0.1 * jax.random.normal(ks[1], (DEPTH, 4, D_MODEL), f32),
        'nsa_w_in': nrm(ks[2], (N_NSA_LAYERS, D_MODEL, NSA_IN_DIM), D_MODEL),
        'nsa_cmp_pos': 0.1 * jax.random.normal(ks[3], (N_NSA_LAYERS, 2, NSA_CMP_LEN, HEAD_DIM), f32),
        'nsa_cmp_w1': nrm(ks[4], (N_NSA_LAYERS, 2, NSA_CMP_LEN * HEAD_DIM, NSA_CMP_HIDDEN), NSA_CMP_LEN * HEAD_DIM),
        'nsa_cmp_w2': nrm(ks[5], (N_NSA_LAYERS, 2, NSA_CMP_HIDDEN, HEAD_DIM), NSA_CMP_HIDDEN),
        'nsa_w_out': nrm(ks[6], (N_NSA_LAYERS, hd, D_MODEL), hd),
        'sb_w_in': nrm(ks[7], (N_SB_LAYERS, D_MODEL, SB_IN_DIM), D_MODEL),
        'sb_w_out': nrm(ks[8], (N_SB_LAYERS, hd, D_MODEL), hd),
        'swa_w_in': nrm(ks[9], (N_SWA_LAYERS, D_MODEL, SWA_IN_DIM), D_MODEL),
        'swa_sinks': 0.5 * jax.random.normal(ks[10], (N_SWA_LAYERS, N_HEADS), f32),
        'swa_w_out': nrm(ks[11], (N_SWA_LAYERS, hd, D_MODEL), hd),
        'ffn_w_up': nrm(ks[12], (DEPTH, D_MODEL, 2 * D_FF), D_MODEL),
        'ffn_conv_w': nrm(ks[13], (DEPTH, CONV_WIDTH, 2 * D_FF), CONV_WIDTH),
        'ffn_conv_b': 0.02 * jax.random.normal(ks[14], (DEPTH, 2 * D_FF), f32),
        'ffn_w_down': nrm(ks[15], (DEPTH, D_FF, D_MODEL), D_FF),
    }


def reference(x, norm_g, nsa_w_in, nsa_cmp_pos, nsa_cmp_w1, nsa_cmp_w2, nsa_w_out,
              sb_w_in, sb_w_out, swa_w_in, swa_sinks, swa_w_out,
              ffn_w_up, ffn_conv_w, ffn_conv_b, ffn_w_down):
    h = x
    for layer in range(DEPTH):
        kind, j = layer % N_MIXERS, layer // N_MIXERS
        u = rms_norm(h, norm_g[layer, 0])
        if kind == 0:
            u = nsa_mixer(u, nsa_w_in[j], nsa_cmp_pos[j], nsa_cmp_w1[j], nsa_cmp_w2[j], nsa_w_out[j])
        elif kind == 1:
            u = stick_breaking_mixer(u, sb_w_in[j], sb_w_out[j])
        else:
            u = swa_sink_mixer(u, swa_w_in[j], swa_sinks[j], swa_w_out[j])
        h = h + rms_norm(u, norm_g[layer, 1])
        u = conv_ffn(rms_norm(h, norm_g[layer, 2]), ffn_w_up[layer], ffn_conv_w[layer], ffn_conv_b[layer], ffn_w_down[layer])
        h = h + rms_norm(u, norm_g[layer, 3])
    return h
```

```python
import functools

import numpy as np
import jax
import jax.numpy as jnp
from jax import lax
from jax.experimental import pallas as pl
from jax.experimental.pallas import tpu as pltpu

F32 = jnp.float32
BF16 = jnp.bfloat16

LANES = 128
VMEM_LIMIT_BYTES = 56 * 1024 * 1024

HEAD_DIM = 64
N_HEADS = 16
Q_BLOCK = 128
RMS_EPS = 1e-6
NEG_INF = -1e30
SEL_FORCE = 1e4
NSA_GROUPS = 4
NSA_CMP_LEN = 32
NSA_CMP_STRIDE = 16
NSA_SEL_LEN = 64
NSA_TOP_N = 16
NSA_WINDOW = 512
SWA_GROUPS = 2
SWA_WINDOW = 128
QK_SCALE = HEAD_DIM ** -0.5

ROW_TILE = 512
FFN_CHUNK = 256
CONV_HALO = 16
SB_KEY_TILE = 256
SEL_KEY_TILE = 512


def _alibi_slope(h):
    return float(2.0 ** (-8.0 * (h + 1) / N_HEADS))


def _params(*semantics):
    return pltpu.CompilerParams(dimension_semantics=semantics, vmem_limit_bytes=VMEM_LIMIT_BYTES)


def _dot(a, b):
    return jnp.dot(a, b, preferred_element_type=F32)


def _dot_nt(a, b):
    return lax.dot_general(a, b, (((1,), (1,)), ((), ())), preferred_element_type=F32)


def _rms(x, g):
    ms = jnp.mean(x * x, axis=-1, keepdims=True)
    return x * lax.rsqrt(ms + RMS_EPS) * g


def _split_bf16(x):
    hi = x.astype(BF16)
    lo = (x - hi.astype(F32)).astype(BF16)
    return hi, lo


def _lane_half_masks(shape):
    lane = lax.broadcasted_iota(jnp.int32, shape, len(shape) - 1)
    return lane < HEAD_DIM, lane >= HEAD_DIM


def _norm_proj_kernel(h_ref, g_ref, *refs):
    n = len(refs) // 2
    xn = _rms(h_ref[...], g_ref[...]).astype(BF16)
    for w_ref, o_ref in zip(refs[:n], refs[n:]):
        width = w_ref.shape[1]
        for c0 in range(0, width, 512):
            c1 = min(c0 + 512, width)
            o_ref[:, c0:c1] = _dot(xn, w_ref[:, c0:c1]).astype(o_ref.dtype)


def norm_proj(h, g, weights, out_dtypes):
    T, D = h.shape
    tm = min(ROW_TILE, T)
    in_specs = [pl.BlockSpec((tm, D), lambda i: (i, 0)), pl.BlockSpec((1, D), lambda i: (0, 0))]
    in_specs += [pl.BlockSpec(w.shape, lambda i: (0, 0)) for w in weights]
    out_specs = [pl.BlockSpec((tm, w.shape[1]), lambda i: (i, 0)) for w in weights]
    out_shape = [jax.ShapeDtypeStruct((T, w.shape[1]), dt) for w, dt in zip(weights, out_dtypes)]
    return pl.pallas_call(
        _norm_proj_kernel, grid=(T // tm,), in_specs=in_specs, out_specs=out_specs,
        out_shape=out_shape, compiler_params=_params("parallel"), name="norm_proj",
    )(h, g.reshape(1, D), *weights)


def _proj_norm_res_kernel(a_ref, w_ref, g_ref, h_ref, o_ref):
    u = _dot(a_ref[...], w_ref[...])
    o_ref[...] = h_ref[...] + _rms(u, g_ref[...])


def proj_norm_res(a, w, g, h):
    T, K = a.shape
    D = w.shape[1]
    tm = min(ROW_TILE, T)
    return pl.pallas_call(
        _proj_norm_res_kernel, grid=(T // tm,),
        in_specs=[pl.BlockSpec((tm, K), lambda i: (i, 0)), pl.BlockSpec((K, D), lambda i: (0, 0)),
                  pl.BlockSpec((1, D), lambda i: (0, 0)), pl.BlockSpec((tm, D), lambda i: (i, 0))],
        out_specs=pl.BlockSpec((tm, D), lambda i: (i, 0)),
        out_shape=jax.ShapeDtypeStruct((T, D), F32),
        compiler_params=_params("parallel"), name="proj_norm_res",
    )(a, w, g.reshape(1, D), h)


def _conv_ffn_kernel(h_ref, halo_ref, g_in_ref, w_up_ref, cw_ref, cb_ref, w_down_ref, g_out_ref,
                     o_ref, xs_ref, ug_ref, uv_ref, acc_ref, *, seq_tiles, d_ff):
    tm = h_ref.shape[0]
    x = h_ref[...]
    g_in = g_in_ref[...]
    xs_ref[CONV_HALO:, :] = _rms(x, g_in).astype(BF16)
    seq_start = (pl.program_id(0) % seq_tiles) == 0
    halo = _rms(halo_ref[...], g_in)
    xs_ref[:CONV_HALO, :] = jnp.where(seq_start, 0.0, halo).astype(BF16)
    acc_ref[...] = jnp.zeros_like(acc_ref)

    def conv(u_ref, col):
        w = cw_ref[:, pl.ds(col, FFN_CHUNK)]
        return (w[0:1] * u_ref[pl.ds(CONV_HALO - 2, tm), :] + w[1:2] * u_ref[pl.ds(CONV_HALO - 1, tm), :]
                + w[2:3] * u_ref[pl.ds(CONV_HALO, tm), :] + cb_ref[:, pl.ds(col, FFN_CHUNK)])

    def chunk(c, carry):
        col_g = pl.multiple_of(c * FFN_CHUNK, FFN_CHUNK)
        col_v = pl.multiple_of(d_ff + c * FFN_CHUNK, FFN_CHUNK)
        xs = xs_ref[...]
        ug_ref[...] = _dot(xs, w_up_ref[:, pl.ds(col_g, FFN_CHUNK)])
        uv_ref[...] = _dot(xs, w_up_ref[:, pl.ds(col_v, FFN_CHUNK)])
        gate = conv(ug_ref, col_g)
        val = conv(uv_ref, col_v)
        act = (gate * jax.nn.sigmoid(gate) * val).astype(BF16)
        acc_ref[...] += _dot(act, w_down_ref[pl.ds(col_g, FFN_CHUNK), :])
        return carry

    lax.fori_loop(0, d_ff // FFN_CHUNK, chunk, 0)
    o_ref[...] = x + _rms(acc_ref[...], g_out_ref[...])


def conv_ffn(h, g_in, w_up, conv_w, conv_b, w_down, g_out, seq_len):
    T, D = h.shape
    d_ff = w_down.shape[0]
    tm = min(ROW_TILE, seq_len)
    halo_blocks = tm // CONV_HALO
    const = lambda i: (0, 0)
    kernel = functools.partial(_conv_ffn_kernel, seq_tiles=seq_len // tm, d_ff=d_ff)
    return pl.pallas_call(
        kernel, grid=(T // tm,),
        in_specs=[pl.BlockSpec((tm, D), lambda i: (i, 0)),
                  pl.BlockSpec((CONV_HALO, D), lambda i: (jnp.maximum(i * halo_blocks - 1, 0), 0)),
                  pl.BlockSpec((1, D), const),
                  pl.BlockSpec(w_up.shape, const, pipeline_mode=pl.Buffered(1)),
                  pl.BlockSpec(conv_w.shape, const),
                  pl.BlockSpec((1, 2 * d_ff), const),
                  pl.BlockSpec(w_down.shape, const, pipeline_mode=pl.Buffered(1)),
                  pl.BlockSpec((1, D), const)],
        out_specs=pl.BlockSpec((tm, D), lambda i: (i, 0)),
        out_shape=jax.ShapeDtypeStruct((T, D), F32),
        scratch_shapes=[pltpu.VMEM((tm + CONV_HALO, D), BF16),
                        pltpu.VMEM((tm + CONV_HALO, FFN_CHUNK), F32),
                        pltpu.VMEM((tm + CONV_HALO, FFN_CHUNK), F32),
                        pltpu.VMEM((tm, D), F32)],
        compiler_params=_params("parallel"), name="conv_ffn",
    )(h, h, g_in.reshape(1, D), w_up, conv_w, conv_b.reshape(1, 2 * d_ff), w_down, g_out.reshape(1, D))


def _sb_kernel(q_ref, k_ref, v_ref, o_ref, acc_ref, carry_ref):
    i = pl.program_id(2)
    tq, tk = Q_BLOCK, SB_KEY_TILE
    lo_half, hi_half = _lane_half_masks((tq, LANES))
    q = q_ref[...] * QK_SCALE
    zero = jnp.zeros_like(q)
    q_heads = (jnp.where(lo_half, q, zero), jnp.where(hi_half, q, zero))
    row = lax.broadcasted_iota(jnp.int32, (tk, tk), 0)
    col = lax.broadcasted_iota(jnp.int32, (tk, tk), 1)
    tri = jnp.where(row >= col, 1.0, 0.0).astype(BF16)
    acc_ref[...] = jnp.zeros_like(acc_ref)
    carry_ref[...] = jnp.zeros_like(carry_ref)

    def tile(start, masked):
        k = k_ref[pl.ds(start, tk), :]
        v = v_ref[pl.ds(start, tk), :]
        if masked:
            t_pos = i * tq + lax.broadcasted_iota(jnp.int32, (tq, tk), 0)
            s_pos = start + lax.broadcasted_iota(jnp.int32, (tq, tk), 1)
            valid = s_pos < t_pos
        for hh in range(2):
            z = _dot_nt(q_heads[hh], k)
            sp = jnp.maximum(z, 0.0) + jnp.log1p(jnp.exp(-jnp.abs(z)))
            if masked:
                sp = jnp.where(valid, sp, 0.0)
            sp_hi, sp_lo = _split_bf16(sp)
            cum = _dot(sp_hi, tri) + _dot(sp_lo, tri)
            a = jnp.exp(z - cum - carry_ref[hh])
            if masked:
                a = jnp.where(valid, a, 0.0)
            acc_ref[hh] += _dot(a.astype(BF16), v)
            carry_ref[hh] += jnp.sum(sp, axis=-1, keepdims=True)

    diag = (i * tq) // tk
    tile(pl.multiple_of(diag * tk, tk), True)

    def body(j, c):
        tile(pl.multiple_of((diag - 1 - j) * tk, tk), False)
        return c

    lax.fori_loop(0, diag, body, 0)
    o_ref[...] = jnp.where(lo_half, acc_ref[0], acc_ref[1]).astype(o_ref.dtype)


def sb_attention(qkv, batch, seq_len):
    T = qkv.shape[0]
    nq = seq_len // Q_BLOCK
    pairs = N_HEADS // 2
    return pl.pallas_call(
        _sb_kernel, grid=(batch, pairs, nq),
        in_specs=[pl.BlockSpec((Q_BLOCK, LANES), lambda b, p, i: (b * nq + i, p)),
                  pl.BlockSpec((seq_len, LANES), lambda b, p, i: (b, pairs + p)),
                  pl.BlockSpec((seq_len, LANES), lambda b, p, i: (b, 2 * pairs + p))],
        out_specs=pl.BlockSpec((Q_BLOCK, LANES), lambda b, p, i: (b * nq + i, p)),
        out_shape=jax.ShapeDtypeStruct((T, N_HEADS * HEAD_DIM), BF16),
        scratch_shapes=[pltpu.VMEM((2, Q_BLOCK, LANES), F32), pltpu.VMEM((2, Q_BLOCK, 1), F32)],
        compiler_params=_params("parallel", "parallel", "arbitrary"), name="sb_attention",
    )(qkv, qkv, qkv)


def _swa_kernel(sink_ref, q_ref, kv_prev_ref, kv_cur_ref, o_ref):
    i = pl.program_id(1)
    tq = Q_BLOCK
    heads_per_group = N_HEADS // SWA_GROUPS
    lo_half, hi_half = _lane_half_masks((tq, LANES))
    halves = (lo_half, hi_half)
    dist_i = (lax.broadcasted_iota(jnp.int32, (tq, 2 * tq), 0)
              - lax.broadcasted_iota(jnp.int32, (tq, 2 * tq), 1) + tq)
    col = lax.broadcasted_iota(jnp.int32, (tq, 2 * tq), 1)
    mask = (dist_i >= 0) & (dist_i < SWA_WINDOW) & ((col >= tq) | (i > 0))
    dist = dist_i.astype(F32)
    for g in range(SWA_GROUPS):
        k = jnp.concatenate([kv_prev_ref[:, g * LANES:(g + 1) * LANES],
                             kv_cur_ref[:, g * LANES:(g + 1) * LANES]], axis=0)
        v0 = (SWA_GROUPS + g) * LANES
        v = jnp.concatenate([kv_prev_ref[:, v0:v0 + LANES], kv_cur_ref[:, v0:v0 + LANES]], axis=0)
        for pp in range(g * heads_per_group // 2, (g + 1) * heads_per_group // 2):
            q = q_ref[:, pp * LANES:(pp + 1) * LANES] * QK_SCALE
            outs = []
            for hh in range(2):
                head = 2 * pp + hh
                sink = sink_ref[0, head]
                s = _dot_nt(jnp.where(halves[hh], q, jnp.zeros_like(q)), k) - _alibi_slope(head) * dist
                s = jnp.where(mask, s, NEG_INF)
                m = jnp.maximum(jnp.max(s, axis=-1, keepdims=True), sink)
                e = jnp.where(mask, jnp.exp(s - m), 0.0)
                p = e / (jnp.sum(e, axis=-1, keepdims=True) + jnp.exp(sink - m))
                outs.append(_dot(p.astype(BF16), v))
            o_ref[:, pp * LANES:(pp + 1) * LANES] = jnp.where(lo_half, outs[0], outs[1]).astype(o_ref.dtype)


def swa_attention(qkv, sinks, batch, seq_len):
    T = qkv.shape[0]
    nq = seq_len // Q_BLOCK
    hd = N_HEADS * HEAD_DIM
    kvw = 2 * SWA_GROUPS * LANES
    kv_blk = hd // kvw
    return pl.pallas_call(
        _swa_kernel, grid=(batch, nq),
        in_specs=[pl.BlockSpec(memory_space=pltpu.SMEM),
                  pl.BlockSpec((Q_BLOCK, hd), lambda b, i: (b * nq + i, 0)),
                  pl.BlockSpec((Q_BLOCK, kvw), lambda b, i: (b * nq + jnp.maximum(i - 1, 0), kv_blk)),
                  pl.BlockSpec((Q_BLOCK, kvw), lambda b, i: (b * nq + i, kv_blk))],
        out_specs=pl.BlockSpec((Q_BLOCK, hd), lambda b, i: (b * nq + i, 0)),
        out_shape=jax.ShapeDtypeStruct((T, hd), BF16),
        compiler_params=_params("parallel", "arbitrary"), name="swa_attention",
    )(sinks.reshape(1, N_HEADS).astype(F32), qkv, qkv, qkv)


def _nsa_compress_kernel(c_ref, pos_ref, w1_ref, w2_ref, o_ref, b_ref):
    n_chunk = c_ref.shape[0]
    c = c_ref[...]
    half = c.shape[1]
    x1 = (c + pos_ref[0:1, :]).astype(BF16)
    x2 = (c + pos_ref[1:2, :]).astype(BF16)
    a = _dot(x1, w1_ref[:half, :])
    b_ref[:n_chunk, :] = _dot(x2, w1_ref[half:, :])
    b_ref[n_chunk:, :] = jnp.zeros((8, b_ref.shape[1]), F32)
    pre = a + b_ref[pl.ds(1, n_chunk), :]
    hid = (pre * jax.nn.sigmoid(pre)).astype(BF16)
    o_ref[...] = _dot(hid, w2_ref[...]).astype(o_ref.dtype)


def nsa_compress(chunks, pos, w1, w2dup):
    _, B, G, n_chunk, width = chunks.shape
    hidden = w1.shape[-1]
    return pl.pallas_call(
        _nsa_compress_kernel, grid=(2, B, G),
        in_specs=[pl.BlockSpec((None, None, None, n_chunk, width), lambda c, b, g: (c, b, g, 0, 0)),
                  pl.BlockSpec((None, 2, width), lambda c, b, g: (c, 0, 0)),
                  pl.BlockSpec((None, 2 * width, hidden), lambda c, b, g: (c, 0, 0)),
                  pl.BlockSpec((None, hidden, LANES), lambda c, b, g: (c, 0, 0))],
        out_specs=pl.BlockSpec((None, None, None, n_chunk, LANES), lambda c, b, g: (c, b, g, 0, 0)),
        out_shape=jax.ShapeDtypeStruct((2, B, G, n_chunk, LANES), BF16),
        scratch_shapes=[pltpu.VMEM((n_chunk + 8, hidden), F32)],
        compiler_params=_params("parallel", "parallel", "parallel"), name="nsa_compress",
    )(chunks, pos, w1, w2dup)


def _nsa_cmp_select_kernel(q_ref, ck_ref, cv_ref, selmap_ref, o_ref, sel_ref):
    i = pl.program_id(1)
    tq = Q_BLOCK
    n_cmp = ck_ref.shape[1]
    n_sel = selmap_ref.shape[0]
    heads_per_group = N_HEADS // NSA_GROUPS
    lo_half, hi_half = _lane_half_masks((tq, LANES))
    halves = (lo_half, hi_half)
    t_row = i * tq + lax.broadcasted_iota(jnp.int32, (tq, n_cmp), 0)
    cmp_end = lax.broadcasted_iota(jnp.int32, (tq, n_cmp), 1) * NSA_CMP_STRIDE + (NSA_CMP_LEN - 1)
    valid = (t_row >= cmp_end) & (lax.broadcasted_iota(jnp.int32, (tq, n_cmp), 1) < n_cmp - 1)
    dist = (t_row - cmp_end).astype(F32)
    blk = lax.broadcasted_iota(jnp.int32, (n_sel, tq), 0)
    t_lane = i * tq + lax.broadcasted_iota(jnp.int32, (n_sel, tq), 1)
    cur = t_lane // NSA_SEL_LEN
    forced = (blk == 0) | (blk == cur) | (blk == cur - 1)
    causal_blk = blk * NSA_SEL_LEN <= t_lane
    blk_f = blk.astype(F32)
    selmap = selmap_ref[...]
    for g in range(NSA_GROUPS):
        ck = ck_ref[g]
        cv = cv_ref[g]
        p_sum = jnp.zeros((tq, n_cmp), F32)
        for pp in range(g * heads_per_group // 2, (g + 1) * heads_per_group // 2):
            q = q_ref[:, pp * LANES:(pp + 1) * LANES] * QK_SCALE
            outs = []
            for hh in range(2):
                head = 2 * pp + hh
                s = _dot_nt(jnp.where(halves[hh], q, jnp.zeros_like(q)), ck) - _alibi_slope(head) * dist
                s = jnp.where(valid, s, NEG_INF)
                m = jnp.max(s, axis=-1, keepdims=True)
                e = jnp.where(valid, jnp.exp(s - m), 0.0)
                p = e / jnp.maximum(jnp.sum(e, axis=-1, keepdims=True), 1e-30)
                p_sum = p_sum + p
                outs.append(_dot(p.astype(BF16), cv))
            o_ref[:, pp * LANES:(pp + 1) * LANES] = jnp.where(lo_half, outs[0], outs[1])
        p_hi, p_lo = _split_bf16(p_sum)
        imp = _dot_nt(selmap, p_hi) + _dot_nt(selmap, p_lo)
        score = jnp.where(forced, SEL_FORCE, jnp.where(causal_blk, imp, NEG_INF))

        def pick(_, state):
            score, chosen = state
            top = jnp.max(score, axis=0, keepdims=True)
            first = jnp.min(jnp.where(score == top, blk_f, float(n_sel)), axis=0, keepdims=True)
            hit = blk_f == first
            return jnp.where(hit, -jnp.inf, score), jnp.where(hit, 1.0, chosen)

        _, chosen = lax.fori_loop(0, NSA_TOP_N, pick, (score, jnp.zeros((n_sel, tq), F32)))
        sel_ref[:, g * n_sel:(g + 1) * n_sel] = chosen.T.astype(sel_ref.dtype)


def nsa_cmp_select(q_all, ck, cv, selmap_t, batch, seq_len):
    T = q_all.shape[0]
    nq = seq_len // Q_BLOCK
    hd = N_HEADS * HEAD_DIM
    n_cmp = ck.shape[2]
    n_sel = seq_len // NSA_SEL_LEN
    return pl.pallas_call(
        _nsa_cmp_select_kernel, grid=(batch, nq),
        in_specs=[pl.BlockSpec((Q_BLOCK, hd), lambda b, i: (b * nq + i, 0)),
                  pl.BlockSpec((None, NSA_GROUPS, n_cmp, LANES), lambda b, i: (b, 0, 0, 0)),
                  pl.BlockSpec((None, NSA_GROUPS, n_cmp, LANES), lambda b, i: (b, 0, 0, 0)),
                  pl.BlockSpec((n_sel, n_cmp), lambda b, i: (0, 0))],
        out_specs=[pl.BlockSpec((Q_BLOCK, hd), lambda b, i: (b * nq + i, 0)),
                   pl.BlockSpec((Q_BLOCK, NSA_GROUPS * n_sel), lambda b, i: (b * nq + i, 0))],
        out_shape=[jax.ShapeDtypeStruct((T, hd), F32),
                   jax.ShapeDtypeStruct((T, NSA_GROUPS * n_sel), BF16)],
        compiler_params=_params("parallel", "arbitrary"), name="nsa_cmp_select",
    )(q_all, ck, cv, selmap_t)


def _nsa_sel_win_kernel(slope_ref, q_ref, sel_ref, expand_ref, ks_ref, vs_ref, kw_ref, vw_ref, ocmp_ref, gate_ref,
                        o_ref, m_ref, l_ref, acc_ref):
    g = pl.program_id(1)
    i = pl.program_id(2)
    tq, tk = Q_BLOCK, SEL_KEY_TILE
    heads_per_group = N_HEADS // NSA_GROUPS
    lo_half, hi_half = _lane_half_masks((tq, LANES))
    halves = (lo_half, hi_half)
    q_heads = []
    for pp in range(heads_per_group // 2):
        q = q_ref[:, pp * LANES:(pp + 1) * LANES] * QK_SCALE
        q_heads += [jnp.where(halves[0], q, jnp.zeros_like(q)), jnp.where(halves[1], q, jnp.zeros_like(q))]
    slopes = [slope_ref[0, g * heads_per_group + r] for r in range(heads_per_group)]
    t_row = i * tq + lax.broadcasted_iota(jnp.int32, (tq, tk), 0)
    col = lax.broadcasted_iota(jnp.int32, (tq, tk), 1)

    m_ref[...] = jnp.full_like(m_ref, NEG_INF)
    l_ref[...] = jnp.zeros_like(l_ref)
    acc_ref[...] = jnp.zeros_like(acc_ref)
    sel = sel_ref[...]

    def sel_tile(j, c):
        start = pl.multiple_of(j * tk, tk)
        k = ks_ref[pl.ds(start, tk), :]
        v = vs_ref[pl.ds(start, tk), :]
        picked = _dot(sel, expand_ref[:, pl.ds(start, tk)])
        dist_i = t_row - (start + col)
        valid = (picked > 0.5) & (dist_i >= 0)
        dist = dist_i.astype(F32)
        for r in range(heads_per_group):
            s = _dot_nt(q_heads[r], k) - slopes[r] * dist
            s = jnp.where(valid, s, NEG_INF)
            m_old = m_ref[r]
            m_new = jnp.maximum(m_old, jnp.max(s, axis=-1, keepdims=True))
            e = jnp.where(valid, jnp.exp(s - m_new), 0.0)
            alpha = jnp.exp(m_old - m_new)
            l_ref[r] = alpha * l_ref[r] + jnp.sum(e, axis=-1, keepdims=True)
            acc_ref[r] = alpha * acc_ref[r] + _dot(e.astype(BF16), v)
            m_ref[r] = m_new
        return c

    lax.fori_loop(0, (i * tq + tq - 1) // tk + 1, sel_tile, 0)

    wk = NSA_WINDOW + tq
    start_w = pl.multiple_of(jnp.maximum(i * tq - NSA_WINDOW, 0), tq)
    kw = kw_ref[pl.ds(start_w, wk), :]
    vw = vw_ref[pl.ds(start_w, wk), :]
    dist_wi = (i * tq + lax.broadcasted_iota(jnp.int32, (tq, wk), 0)
               - (start_w + lax.broadcasted_iota(jnp.int32, (tq, wk), 1)))
    valid_w = (dist_wi >= 0) & (dist_wi < NSA_WINDOW)
    dist_w = dist_wi.astype(F32)
    gates = jax.nn.sigmoid(gate_ref[...])
    for pp in range(heads_per_group // 2):
        mixed = []
        for hh in range(2):
            r = 2 * pp + hh
            s = _dot_nt(q_heads[r], kw) - slopes[r] * dist_w
            s = jnp.where(valid_w, s, NEG_INF)
            m = jnp.max(s, axis=-1, keepdims=True)
            e = jnp.where(valid_w, jnp.exp(s - m), 0.0)
            p = e / jnp.maximum(jnp.sum(e, axis=-1, keepdims=True), 1e-30)
            o_win = _dot(p.astype(BF16), vw)
            o_sel = acc_ref[r] / jnp.maximum(l_ref[r], 1e-30)
            gcol = 3 * r
            mixed.append(gates[:, gcol:gcol + 1] * ocmp_ref[:, pp * LANES:(pp + 1) * LANES]
                         + gates[:, gcol + 1:gcol + 2] * o_sel + gates[:, gcol + 2:gcol + 3] * o_win)
        o_ref[:, pp * LANES:(pp + 1) * LANES] = jnp.where(lo_half, mixed[0], mixed[1]).astype(o_ref.dtype)


def nsa_sel_win(proj, sel, expand, ocmp, gates, batch, seq_len):
    T = proj.shape[0]
    nq = seq_len // Q_BLOCK
    hd = N_HEADS * HEAD_DIM
    G = NSA_GROUPS
    gw = hd // G
    n_sel = seq_len // NSA_SEL_LEN
    qcols = hd // LANES
    row = lambda b, g, i: b * nq + i
    kv_spec = lambda part: pl.BlockSpec((seq_len, LANES), lambda b, g, i: (b, qcols + part * G + g))
    slopes = jnp.asarray([[_alibi_slope(hh) for hh in range(N_HEADS)]], dtype=F32)
    return pl.pallas_call(
        _nsa_sel_win_kernel, grid=(batch, G, nq),
        in_specs=[pl.BlockSpec(memory_space=pltpu.SMEM),
                  pl.BlockSpec((Q_BLOCK, gw), lambda b, g, i: (row(b, g, i), g)),
                  pl.BlockSpec((Q_BLOCK, n_sel), lambda b, g, i: (row(b, g, i), g)),
                  pl.BlockSpec((n_sel, seq_len), lambda b, g, i: (0, 0)),
                  kv_spec(0), kv_spec(1), kv_spec(2), kv_spec(3),
                  pl.BlockSpec((Q_BLOCK, gw), lambda b, g, i: (row(b, g, i), g)),
                  pl.BlockSpec((Q_BLOCK, LANES), lambda b, g, i: (row(b, g, i), g))],
        out_specs=pl.BlockSpec((Q_BLOCK, gw), lambda b, g, i: (row(b, g, i), g)),
        out_shape=jax.ShapeDtypeStruct((T, hd), BF16),
        scratch_shapes=[pltpu.VMEM((N_HEADS // G, Q_BLOCK, 1), F32),
                        pltpu.VMEM((N_HEADS // G, Q_BLOCK, 1), F32),
                        pltpu.VMEM((N_HEADS // G, Q_BLOCK, LANES), F32)],
        compiler_params=_params("parallel", "parallel", "arbitrary"), name="nsa_sel_win",
    )(slopes, proj, sel, expand, proj, proj, proj, proj, ocmp, gates)


def _dup_cols(w, groups):
    d = w.shape[0]
    w = w.reshape(d, groups, 1, HEAD_DIM)
    return jnp.broadcast_to(w, (d, groups, 2, HEAD_DIM)).reshape(d, groups * LANES)


def _nsa_layer(h, g_pre, g_post, w_in, cmp_pos, cmp_w1, cmp_w2, w_out, batch, seq_len):
    T, D = h.shape
    G, dh = NSA_GROUPS, HEAD_DIM
    hd, kd = N_HEADS * dh, NSA_GROUPS * dh
    cuts = [hd + c * kd for c in range(7)]
    w_q, w_kc, w_vc, w_ks, w_vs, w_kw, w_vw, w_g = jnp.split(w_in, cuts, axis=1)
    w_main = jnp.concatenate([w_q] + [_dup_cols(w, G) for w in (w_ks, w_vs, w_kw, w_vw)], axis=1).astype(BF16)
    heads_per_group = N_HEADS // G
    w_gate = w_g.reshape(D, G, 3 * heads_per_group)
    w_gate = jnp.pad(w_gate, ((0, 0), (0, 0), (0, LANES - 3 * heads_per_group))).reshape(D, G * LANES)
    w_side = jnp.concatenate([w_kc, w_vc, w_gate], axis=1).astype(BF16)
    proj, side = norm_proj(h, g_pre, [w_main, w_side], [BF16, F32])

    n_chunk = seq_len // NSA_CMP_STRIDE
    cmp_in = side[:, :2 * kd].reshape(batch, n_chunk, NSA_CMP_STRIDE, 2, G, dh)
    chunks = cmp_in.transpose(3, 0, 4, 1, 2, 5).reshape(2, batch, G, n_chunk, NSA_CMP_STRIDE * dh)
    pos = cmp_pos.reshape(2, 2, NSA_CMP_STRIDE * dh)
    w2dup = jnp.concatenate([cmp_w2, cmp_w2], axis=-1).astype(BF16)
    ckv = nsa_compress(chunks, pos, cmp_w1.astype(BF16), w2dup)

    n_sel = seq_len // NSA_SEL_LEN
    cmp_start = np.arange(n_chunk) * NSA_CMP_STRIDE
    sel_start = np.arange(n_sel) * NSA_SEL_LEN
    overlap = ((cmp_start[None, :] < sel_start[:, None] + NSA_SEL_LEN)
               & (cmp_start[None, :] + NSA_CMP_LEN > sel_start[:, None]))
    overlap[:, n_chunk - 1] = False
    selmap_t = jnp.asarray(overlap.astype(np.float32), dtype=BF16)
    ocmp, sel = nsa_cmp_select(proj, ckv[0], ckv[1], selmap_t, batch, seq_len)

    expand = jnp.asarray((np.arange(seq_len)[None, :] // NSA_SEL_LEN == np.arange(n_sel)[:, None])
                         .astype(np.float32), dtype=BF16)
    mixed = nsa_sel_win(proj, sel, expand, ocmp, side[:, 2 * kd:], batch, seq_len)
    return proj_norm_res(mixed, w_out.astype(BF16), g_post, h)


def _sb_layer(h, g_pre, g_post, w_in, w_out, batch, seq_len):
    (qkv,) = norm_proj(h, g_pre, [w_in.astype(BF16)], [BF16])
    mixed = sb_attention(qkv, batch, seq_len)
    return proj_norm_res(mixed, w_out.astype(BF16), g_post, h)


def _swa_layer(h, g_pre, g_post, w_in, sinks, w_out, batch, seq_len):
    hd, kd = N_HEADS * HEAD_DIM, SWA_GROUPS * HEAD_DIM
    w_q, w_k, w_v = jnp.split(w_in, [hd, hd + kd], axis=1)
    w = jnp.concatenate([w_q, _dup_cols(w_k, SWA_GROUPS), _dup_cols(w_v, SWA_GROUPS)], axis=1).astype(BF16)
    (qkv,) = norm_proj(h, g_pre, [w], [BF16])
    mixed = swa_attention(qkv, sinks, batch, seq_len)
    return proj_norm_res(mixed, w_out.astype(BF16), g_post, h)


def kernel(x, norm_g, nsa_w_in, nsa_cmp_pos, nsa_cmp_w1, nsa_cmp_w2, nsa_w_out, sb_w_in, sb_w_out,
           swa_w_in, swa_sinks, swa_w_out, ffn_w_up, ffn_conv_w, ffn_conv_b, ffn_w_down):
    batch, seq_len, d_model = x.shape
    depth = norm_g.shape[0]
    h = x.reshape(batch * seq_len, d_model)
    for layer in range(depth):
        kind, j = layer % 3, layer // 3
        g = norm_g[layer]
        if kind == 0:
            h = _nsa_layer(h, g[0], g[1], nsa_w_in[j], nsa_cmp_pos[j], nsa_cmp_w1[j], nsa_cmp_w2[j],
                           nsa_w_out[j], batch, seq_len)
        elif kind == 1:
            h = _sb_layer(h, g[0], g[1], sb_w_in[j], sb_w_out[j], batch, seq_len)
        else:
            h = _swa_layer(h, g[0], g[1], swa_w_in[j], swa_sinks[j], swa_w_out[j], batch, seq_len)
        h = conv_ffn(h, g[2], ffn_w_up[layer].astype(BF16), ffn_conv_w[layer], ffn_conv_b[layer],
                     ffn_w_down[layer].astype(BF16), g[3], seq_len)
    return h.reshape(batch, seq_len, d_model)
```
